```python
import jax, jax.numpy as jnp
from jax import lax
import numpy as np

D_MODEL = 1024
BATCH = 4
SEQ = 8192
DEPTH = 2

CHUNK = 64
D_BRANCH = 768
N_BRANCH = 3
CONV_K = 31
GMLP_BLOCK = 128
GMLP_HEADS = 4
GMLP_HD = D_BRANCH // GMLP_HEADS
POOL_WINDOWS = (2, 4, 8, 16)
POOL_GROUPS = 4
POOL_GD = D_BRANCH // POOL_GROUPS
D_FF = 2 * D_MODEL
FFN_K = 3
EPS = 1e-6
OFF_CONV = 0
OFF_GMLP = 2 * D_BRANCH
OFF_POOL = 4 * D_BRANCH
OFF_GATE = 5 * D_BRANCH
N_IN = 5 * D_BRANCH + N_BRANCH * D_MODEL

kernel_name = "hybrid_conv_gmlp_pool_encoder"


def rmsnorm(x, g):
    xf = x.astype(jnp.float32)
    y = xf * lax.rsqrt(jnp.mean(xf * xf, axis=-1, keepdims=True) + EPS)
    return (y * g.astype(jnp.float32)).astype(x.dtype)


def layernorm(x, g, b):
    xf = x.astype(jnp.float32)
    mu = jnp.mean(xf, axis=-1, keepdims=True)
    var = jnp.mean(jnp.square(xf - mu), axis=-1, keepdims=True)
    y = (xf - mu) * lax.rsqrt(var + EPS)
    return (y * g.astype(jnp.float32) + b.astype(jnp.float32)).astype(x.dtype)


def dwconv_causal(x, w):
    k, c = w.shape
    filt = w.astype(x.dtype)[:, None, :]
    return lax.conv_general_dilated(
        x, filt, window_strides=(1,), padding=[(k - 1, 0)],
        dimension_numbers=("NWC", "WIO", "NWC"), feature_group_count=c)


def multiscale_pool(p, pool_w, pool_scale):
    b, s, _ = p.shape
    pf = p.astype(jnp.float32)
    c = jnp.cumsum(pf, axis=1)
    pos = (jnp.arange(s) + 1)[None, :, None]
    outs = []
    for g, w in enumerate(POOL_WINDOWS):
        cg = c[..., g * POOL_GD:(g + 1) * POOL_GD]
        prev = jnp.pad(cg[:, :-w], ((0, 0), (w, 0), (0, 0)))
        cnt = jnp.minimum(pos, w).astype(jnp.float32)
        outs.append((cg - prev) / cnt)
    pooled = (jnp.concatenate(outs, axis=-1) - pf).astype(p.dtype)
    pooled = pooled.reshape(b, s, POOL_GROUPS, POOL_GD)
    mixed = jnp.einsum("bsgc,gcd->bsgd", pooled, pool_w).reshape(b, s, D_BRANCH)
    return mixed * pool_scale


def hybrid_mixer(xn, w_in, b_in, conv_w, conv_ln_g, conv_ln_b, conv_proj,
                 gmlp_ln_g, gmlp_ln_b, gmlp_ws, gmlp_bs, gmlp_proj,
                 pool_w, pool_scale, pool_proj, w_out):
    b, s, _ = xn.shape
    z = xn @ w_in + b_in
    zc = z[..., OFF_CONV:OFF_GMLP]
    zg = z[..., OFF_GMLP:OFF_POOL]
    zp = z[..., OFF_POOL:OFF_GATE]
    zgate = z[..., OFF_GATE:]

    a = zc[..., :D_BRANCH] * jax.nn.sigmoid(zc[..., D_BRANCH:])
    a = dwconv_causal(a, conv_w)
    a = jax.nn.silu(layernorm(a, conv_ln_g, conv_ln_b))
    h_a = a @ conv_proj

    zg = jax.nn.gelu(zg)
    u, v = zg[..., :D_BRANCH], zg[..., D_BRANCH:]
    v = layernorm(v, gmlp_ln_g, gmlp_ln_b)
    v = v.reshape(b, s // GMLP_BLOCK, GMLP_BLOCK, GMLP_HEADS, GMLP_HD)
    blk = jnp.arange(GMLP_BLOCK) // CHUNK
    mask = (blk[None, :] <= blk[:, None]).astype(gmlp_ws.dtype)
    ws = gmlp_ws * mask[None]
    sv = jnp.einsum("hij,bnjhd->bnihd", ws, v) + gmlp_bs.T[:, :, None]
    h_b = (u * sv.reshape(b, s, D_BRANCH)) @ gmlp_proj

    h_c = multiscale_pool(zp, pool_w, pool_scale) @ pool_proj

    gates = jax.nn.sigmoid(zgate).reshape(b, s, N_BRANCH, D_MODEL)
    y = gates[..., 0, :] * h_a + gates[..., 1, :] * h_b + gates[..., 2, :] * h_c
    return y @ w_out


def conv_ffn(xn, w_up, ffn_conv_w, w_down):
    h = xn @ w_up
    h = dwconv_causal(h, ffn_conv_w)
    g, val = h[..., :D_FF], h[..., D_FF:]
    return (jax.nn.gelu(g) * val) @ w_down


def setup_inputs(seed: int = 0) -> dict:
    key = jax.random.key(seed)
    ks = iter(jax.random.split(key, 32))
    f32 = jnp.float32

    def nrm(shape, scale):
        return jax.random.normal(next(ks), shape, f32) * scale

    L = DEPTH
    return {
        "x": nrm((BATCH, SEQ, D_MODEL), 1.0),
        "attn_norm_g": 1.0 + nrm((L, D_MODEL), 0.02),
        "w_in": nrm((L, D_MODEL, N_IN), D_MODEL ** -0.5),
        "b_in": nrm((L, N_IN), 0.02),
        "conv_w": nrm((L, CONV_K, D_BRANCH), CONV_K ** -0.5),
        "conv_ln_g": 1.0 + nrm((L, D_BRANCH), 0.02),
        "conv_ln_b": nrm((L, D_BRANCH), 0.02),
        "conv_proj": nrm((L, D_BRANCH, D_MODEL), D_BRANCH ** -0.5),
        "gmlp_ln_g": 1.0 + nrm((L, D_BRANCH), 0.02),
        "gmlp_ln_b": nrm((L, D_BRANCH), 0.02),
        "gmlp_ws": nrm((L, GMLP_HEADS, GMLP_BLOCK, GMLP_BLOCK), GMLP_BLOCK ** -0.5),
        "gmlp_bs": 1.0 + nrm((L, GMLP_HEADS, GMLP_BLOCK), 0.02),
        "gmlp_proj": nrm((L, D_BRANCH, D_MODEL), D_BRANCH ** -0.5),
        "pool_w": nrm((L, POOL_GROUPS, POOL_GD, POOL_GD), POOL_GD ** -0.5),
        "pool_scale": 1.0 + nrm((L, D_BRANCH), 0.02),
        "pool_proj": nrm((L, D_BRANCH, D_MODEL), D_BRANCH ** -0.5),
        "w_out": nrm((L, D_MODEL, D_MODEL), D_MODEL ** -0.5),
        "ffn_norm_g": 1.0 + nrm((L, D_MODEL), 0.02),
        "w_up": nrm((L, D_MODEL, 2 * D_FF), D_MODEL ** -0.5),
        "ffn_conv_w": nrm((L, FFN_K, 2 * D_FF), FFN_K ** -0.5),
        "w_down": nrm((L, D_FF, D_MODEL), D_FF ** -0.5),
        "final_norm_g": 1.0 + nrm((D_MODEL,), 0.02),
    }


def reference(x, attn_norm_g, w_in, b_in, conv_w, conv_ln_g, conv_ln_b, conv_proj,
              gmlp_ln_g, gmlp_ln_b, gmlp_ws, gmlp_bs, gmlp_proj,
              pool_w, pool_scale, pool_proj, w_out,
              ffn_norm_g, w_up, ffn_conv_w, w_down, final_norm_g):
    for l in range(DEPTH):
        h = rmsnorm(x, attn_norm_g[l])
        x = x + hybrid_mixer(h, w_in[l], b_in[l], conv_w[l], conv_ln_g[l], conv_ln_b[l],
                             conv_proj[l], gmlp_ln_g[l], gmlp_ln_b[l], gmlp_ws[l],
                             gmlp_bs[l], gmlp_proj[l], pool_w[l], pool_scale[l],
                             pool_proj[l], w_out[l])
        h = rmsnorm(x, ffn_norm_g[l])
        x = x + conv_ffn(h, w_up[l], ffn_conv_w[l], w_down[l])
    return rmsnorm(x, final_norm_g)
```

```python
import functools

import jax
import jax.numpy as jnp
from jax import lax
from jax.experimental import pallas as pl
from jax.experimental.pallas import tpu as pltpu

D_MODEL = 1024
D_BRANCH = 768
CONV_K = 31
GMLP_BLOCK = 128
GMLP_HEADS = 4
GMLP_HD = D_BRANCH // GMLP_HEADS
CHUNK = 64
POOL_WINDOWS = (2, 4, 8, 16)
POOL_GD = D_BRANCH // len(POOL_WINDOWS)
D_FF = 2 * D_MODEL
FFN_K = 3
EPS = 1e-6
OFF_GMLP = 2 * D_BRANCH
OFF_POOL = 4 * D_BRANCH
OFF_GATE = 5 * D_BRANCH
N_IN = 5 * D_BRANCH + 3 * D_MODEL

LANES = 128
N_BT = D_BRANCH // LANES
N_FT = 2 * D_FF // LANES
CONV_HALO = 32
POOL_HALO = 16
FFN_HALO = 8
MXU_N = 256
ROWS = 256
VMEM_LIMIT = 56 * 1024 * 1024


def _rmsnorm(x, g):
    ms = jnp.mean(x * x, axis=-1, keepdims=True)
    return x * lax.rsqrt(ms + EPS) * g


def _layernorm(x, g, b):
    mu = jnp.mean(x, axis=-1, keepdims=True)
    xc = x - mu
    var = jnp.mean(xc * xc, axis=-1, keepdims=True)
    return xc * lax.rsqrt(var + EPS) * g + b


def _dot(a, b):
    return jnp.dot(a, b, preferred_element_type=jnp.float32)


def _pool_window_of_lane_tile(j):
    lo, hi = j * LANES, (j + 1) * LANES
    out = []
    for g, w in enumerate(POOL_WINDOWS):
        a, b = max(lo, g * POOL_GD), min(hi, (g + 1) * POOL_GD)
        if a < b:
            out.append((w, a - lo))
    return out


def _mixer_kernel(x_ref, ng_ref, w_in_ref, b_in_ref, cw_ref, clg_ref, clb_ref, cproj_ref,
                  glg_ref, glb_ref, ws_ref, bsf_ref, gproj_ref,
                  pw_ref, pscale_ref, winv_ref, icnt_ref, pproj_ref, wout_ref,
                  o_ref, abuf, pbuf):
    rows = x_ref.shape[1]
    s = pl.program_id(1)

    @pl.when(s == 0)
    def _():
        abuf[:, 0:CONV_HALO, :] = jnp.zeros((N_BT, CONV_HALO, LANES), jnp.float32)
        pbuf[:, 0:POOL_HALO, :] = jnp.zeros((N_BT, POOL_HALO, LANES), jnp.float32)

    x = x_ref[0]
    xn = _rmsnorm(x, ng_ref[...]).astype(jnp.bfloat16)

    zc = _dot(xn, w_in_ref[:, 0:OFF_GMLP]) + b_in_ref[:, 0:OFF_GMLP]
    a = zc[:, :D_BRANCH] * jax.nn.sigmoid(zc[:, D_BRANCH:])
    conv_tiles = []
    for j in range(N_BT):
        abuf[j, CONV_HALO:CONV_HALO + rows, :] = a[:, j * LANES:(j + 1) * LANES]
        acc = None
        for k in range(CONV_K):
            off = CONV_HALO - (CONV_K - 1) + k
            term = abuf[j, pl.ds(off, rows), :] * cw_ref[k:k + 1, j * LANES:(j + 1) * LANES]
            acc = term if acc is None else acc + term
        conv_tiles.append(acc)
        abuf[j, 0:CONV_HALO, :] = abuf[j, rows:rows + CONV_HALO, :]
    c = jnp.concatenate(conv_tiles, axis=-1)
    c = jax.nn.silu(_layernorm(c, clg_ref[...], clb_ref[...]))
    h_a = _dot(c.astype(jnp.bfloat16), cproj_ref[...])

    zg = jax.nn.gelu(_dot(xn, w_in_ref[:, OFF_GMLP:OFF_POOL]) + b_in_ref[:, OFF_GMLP:OFF_POOL])
    u = zg[:, :D_BRANCH]
    v = _layernorm(zg[:, D_BRANCH:], glg_ref[...], glb_ref[...]).astype(jnp.bfloat16)
    ri = lax.broadcasted_iota(jnp.int32, (GMLP_BLOCK, GMLP_BLOCK), 0) // CHUNK
    ci = lax.broadcasted_iota(jnp.int32, (GMLP_BLOCK, GMLP_BLOCK), 1) // CHUNK
    wsm = [jnp.where(ci <= ri, ws_ref[h], 0.0).astype(jnp.bfloat16) for h in range(GMLP_HEADS)]
    lane = lax.broadcasted_iota(jnp.int32, (GMLP_BLOCK, MXU_N), 1)
    sv_rows = []
    for n in range(rows // GMLP_BLOCK):
        sv_tiles = []
        for t in range(D_BRANCH // MXU_N):
            h0 = (t * MXU_N) // GMLP_HD
            split = (h0 + 1) * GMLP_HD - t * MXU_N
            vt = v[n * GMLP_BLOCK:(n + 1) * GMLP_BLOCK, t * MXU_N:(t + 1) * MXU_N]
            zero = jnp.zeros_like(vt)
            rhs = jnp.concatenate([jnp.where(lane < split, vt, zero),
                                   jnp.where(lane < split, zero, vt)], axis=0)
            lhs = jnp.concatenate([wsm[h0], wsm[h0 + 1]], axis=1)
            sv_tiles.append(_dot(lhs, rhs))
        sv_rows.append(jnp.concatenate(sv_tiles, axis=1) + bsf_ref[...])
    sv = jnp.concatenate(sv_rows, axis=0)
    h_b = _dot((u * sv).astype(jnp.bfloat16), gproj_ref[...])

    zp = _dot(xn, w_in_ref[:, OFF_POOL:OFF_GATE]) + b_in_ref[:, OFF_POOL:OFF_GATE]
    lane1 = lax.broadcasted_iota(jnp.int32, (rows, LANES), 1)
    wsum_tiles = []
    for j in range(N_BT):
        pbuf[j, POOL_HALO:POOL_HALO + rows, :] = zp[:, j * LANES:(j + 1) * LANES]
        parts = _pool_window_of_lane_tile(j)
        wmax = max(w for w, _ in parts)
        run = None
        sums = {}
        for i in range(wmax):
            term = pbuf[j, pl.ds(POOL_HALO - i, rows), :]
            run = term if run is None else run + term
            sums[i + 1] = run
        tile = sums[parts[0][0]]
        for w, first in parts[1:]:
            tile = jnp.where(lane1 < first, tile, sums[w])
        wsum_tiles.append(tile)
        pbuf[j, 0:POOL_HALO, :] = pbuf[j, rows:rows + POOL_HALO, :]
    wsum = jnp.concatenate(wsum_tiles, axis=-1)
    head_inv = jnp.where(s == 0, icnt_ref[...], jnp.broadcast_to(winv_ref[...], (POOL_HALO, D_BRANCH)))
    avg = jnp.concatenate([wsum[:POOL_HALO] * head_inv, wsum[POOL_HALO:] * winv_ref[...]], axis=0)
    pooled = (avg - zp).astype(jnp.bfloat16)
    mixed = _dot(pooled, pw_ref[...]) * pscale_ref[...]
    h_c = _dot(mixed.astype(jnp.bfloat16), pproj_ref[...])

    gates = jax.nn.sigmoid(_dot(xn, w_in_ref[:, OFF_GATE:N_IN]) + b_in_ref[:, OFF_GATE:N_IN])
    y = (gates[:, 0:D_MODEL] * h_a + gates[:, D_MODEL:2 * D_MODEL] * h_b
         + gates[:, 2 * D_MODEL:3 * D_MODEL] * h_c)
    o_ref[0] = x + _dot(y.astype(jnp.bfloat16), wout_ref[...])


def _ffn_kernel(x_ref, ng_ref, wup_ref, fcw_ref, wdown_ref, fg_ref, o_ref, hbuf, *, final_norm):
    rows = x_ref.shape[1]
    s = pl.program_id(1)

    @pl.when(s == 0)
    def _():
        hbuf[:, 0:FFN_HALO, :] = jnp.zeros((N_FT, FFN_HALO, LANES), jnp.float32)

    x = x_ref[0]
    xn = _rmsnorm(x, ng_ref[...]).astype(jnp.bfloat16)
    h = _dot(xn, wup_ref[...])
    conv_tiles = []
    for j in range(N_FT):
        hbuf[j, FFN_HALO:FFN_HALO + rows, :] = h[:, j * LANES:(j + 1) * LANES]
        acc = None
        for k in range(FFN_K):
            off = FFN_HALO - (FFN_K - 1) + k
            term = hbuf[j, pl.ds(off, rows), :] * fcw_ref[k:k + 1, j * LANES:(j + 1) * LANES]
            acc = term if acc is None else acc + term
        conv_tiles.append(acc)
        hbuf[j, 0:FFN_HALO, :] = hbuf[j, rows:rows + FFN_HALO, :]
    hc = jnp.concatenate(conv_tiles, axis=-1)
    act = (jax.nn.gelu(hc[:, :D_FF]) * hc[:, D_FF:]).astype(jnp.bfloat16)
    out = x + _dot(act, wdown_ref[...])
    if final_norm:
        out = _rmsnorm(out, fg_ref[...])
    o_ref[0] = out


def _const_spec(shape):
    return pl.BlockSpec(shape, lambda b, s: (0,) * len(shape), pipeline_mode=pl.Buffered(1))


def _call(body, x, consts, scratch, name):
    batch, seq, d = x.shape
    act_spec = pl.BlockSpec((1, ROWS, d), lambda b, s: (b, s, 0))
    return pl.pallas_call(
        body,
        grid=(batch, seq // ROWS),
        in_specs=[act_spec] + [_const_spec(c.shape) for c in consts],
        out_specs=act_spec,
        out_shape=jax.ShapeDtypeStruct(x.shape, x.dtype),
        scratch_shapes=scratch,
        compiler_params=pltpu.CompilerParams(
            dimension_semantics=("arbitrary", "arbitrary"),
            vmem_limit_bytes=VMEM_LIMIT),
        name=name,
    )(x, *consts)


def _pool_tables():
    w_of_ch = jnp.repeat(jnp.asarray(POOL_WINDOWS, jnp.float32), POOL_GD)[None, :]
    pos1 = jnp.arange(1, POOL_HALO + 1, dtype=jnp.float32)[:, None]
    return 1.0 / w_of_ch, 1.0 / jnp.minimum(pos1, w_of_ch)


def kernel(x, attn_norm_g, w_in, b_in, conv_w, conv_ln_g, conv_ln_b, conv_proj, gmlp_ln_g, gmlp_ln_b, gmlp_ws, gmlp_bs, gmlp_proj, pool_w, pool_scale, pool_proj, w_out, ffn_norm_g, w_up, ffn_conv_w, w_down, final_norm_g):
    depth = w_in.shape[0]
    bf = jnp.bfloat16
    winv, icnt = _pool_tables()
    row = lambda v: v[None, :]
    for l in range(depth):
        bsf = jnp.repeat(gmlp_bs[l].T, GMLP_HD, axis=1)
        pw = jax.scipy.linalg.block_diag(*[pool_w[l, g] for g in range(len(POOL_WINDOWS))])
        mixer_consts = [
            row(attn_norm_g[l]), w_in[l].astype(bf), row(b_in[l]), conv_w[l],
            row(conv_ln_g[l]), row(conv_ln_b[l]), conv_proj[l].astype(bf),
            row(gmlp_ln_g[l]), row(gmlp_ln_b[l]), gmlp_ws[l], bsf, gmlp_proj[l].astype(bf),
            pw.astype(bf), row(pool_scale[l]), winv, icnt, pool_proj[l].astype(bf),
            w_out[l].astype(bf)]
        x = _call(_mixer_kernel, x, mixer_consts,
                  [pltpu.VMEM((N_BT, ROWS + CONV_HALO, LANES), jnp.float32),
                   pltpu.VMEM((N_BT, ROWS + POOL_HALO, LANES), jnp.float32)],
                  name=f"mixer{l}")
        ffn_consts = [row(ffn_norm_g[l]), w_up[l].astype(bf), ffn_conv_w[l],
                      w_down[l].astype(bf), row(final_norm_g)]
        x = _call(functools.partial(_ffn_kernel, final_norm=(l == depth - 1)), x, ffn_consts,
                  [pltpu.VMEM((N_FT, ROWS + FFN_HALO, LANES), jnp.float32)],
                  name=f"ffn{l}")
    return x
```

```python
import functools

import jax
import jax.numpy as jnp
from jax import lax
from jax.experimental import pallas as pl
from jax.experimental.pallas import tpu as pltpu

D_MODEL = 1024
D_BRANCH = 768
CONV_K = 31
GMLP_BLOCK = 128
GMLP_HEADS = 4
GMLP_HD = D_BRANCH // GMLP_HEADS
CHUNK = 64
POOL_WINDOWS = (2, 4, 8, 16)
POOL_GD = D_BRANCH // len(POOL_WINDOWS)
D_FF = 2 * D_MODEL
FFN_K = 3
EPS = 1e-6
OFF_GMLP = 2 * D_BRANCH
OFF_POOL = 4 * D_BRANCH
OFF_GATE = 5 * D_BRANCH
N_IN = 5 * D_BRANCH + 3 * D_MODEL

LANES = 128
N_BT = D_BRANCH // LANES
N_FT = 2 * D_FF // LANES
CONV_HALO = 32
POOL_HALO = 16
FFN_HALO = 8
MXU_N = 256
ROWS = 256
CONV_ROWS = 64
VMEM_LIMIT = 56 * 1024 * 1024


def _rmsnorm(x, g):
    ms = jnp.mean(x * x, axis=-1, keepdims=True)
    return x * lax.rsqrt(ms + EPS) * g


def _layernorm(x, g, b):
    mu = jnp.mean(x, axis=-1, keepdims=True)
    xc = x - mu
    var = jnp.mean(xc * xc, axis=-1, keepdims=True)
    return xc * lax.rsqrt(var + EPS) * g + b


_dot = functools.partial(jnp.dot, preferred_element_type=jnp.float32)


def _exact_zero_of(x, shape):
    r, w = x.shape
    bits = lax.bitcast_convert_type(x, jnp.uint32)
    tiles = [bits[i:i + 8, l:l + LANES] for i in range(0, r, 8) for l in range(0, w, LANES)]
    while len(tiles) > 1:
        tiles = [a | b for a, b in zip(tiles[0::2], tiles[1::2])] + tiles[len(tiles) & ~1:]
    zero = lax.bitcast_convert_type((tiles[0] >> 16) >> 16, jnp.float32)
    return jnp.tile(zero, (shape[0] // 8, shape[1] // LANES))


def _pool_window_of_lane_tile(j):
    lo, hi = j * LANES, (j + 1) * LANES
    out = []
    for g, w in enumerate(POOL_WINDOWS):
        a, b = max(lo, g * POOL_GD), min(hi, (g + 1) * POOL_GD)
        if a < b:
            out.append((w, a - lo))
    return out


def _mixer_kernel(x_ref, ng_ref, w_in_ref, wgate_ref, b_in_ref, cw_ref, clg_ref, clb_ref, cproj_ref,
                  glg_ref, glb_ref, ws_ref, bsf_ref, gproj_ref,
                  pw_ref, pscale_ref, winv_ref, icnt_ref, pproj_ref, wout_ref,
                  o_ref, abuf, pbuf, zgate_ref, xn_ref):
    rows = x_ref.shape[1]
    s = pl.program_id(1)

    @pl.when(s == 0)
    def _():
        abuf[:, 0:CONV_HALO, :] = jnp.zeros((N_BT, CONV_HALO, LANES), jnp.float32)
        pbuf[:, 0:POOL_HALO, :] = jnp.zeros((N_BT, POOL_HALO, LANES), jnp.float32)

    x = x_ref[0]
    xn_ref[...] = _rmsnorm(x, ng_ref[...]).astype(jnp.bfloat16)
    xn = xn_ref[...]

    zc = _dot(xn, w_in_ref[:, 0:OFF_GMLP]) + b_in_ref[:, 0:OFF_GMLP]
    a = zc[:, :D_BRANCH] * jax.nn.sigmoid(zc[:, D_BRANCH:])
    for j in range(N_BT):
        abuf[j, CONV_HALO:CONV_HALO + rows, :] = a[:, j * LANES:(j + 1) * LANES]
    gate_w = (N_IN - OFF_GATE) // N_BT

    conv_tiles = []
    for j in range(N_BT):
        zgate = _dot(xn, wgate_ref[j])
        zgate_ref[j] = zgate
        chunks = []
        for r in range(0, rows, CONV_ROWS):
            acc = None
            for k in range(CONV_K):
                off = CONV_HALO - (CONV_K - 1) + k + r
                term = abuf[j, pl.ds(off, CONV_ROWS), :] * cw_ref[j, k:k + 1, :]
                acc = term if acc is None else acc + term
            chunks.append(acc)
        chunks[0] = chunks[0] + _exact_zero_of(zgate, chunks[0].shape)
        conv_tiles.append(jnp.concatenate(chunks, axis=0))
        abuf[j, 0:CONV_HALO, :] = abuf[j, rows:rows + CONV_HALO, :]

    zg = _dot(xn, w_in_ref[:, OFF_GMLP:OFF_POOL]) + b_in_ref[:, OFF_GMLP:OFF_POOL]
    c = jnp.concatenate(conv_tiles, axis=-1)
    c = jax.nn.silu(_layernorm(c, clg_ref[...], clb_ref[...])).astype(jnp.bfloat16)
    for j in range(N_BT):
        zgate_ref[j] = jax.nn.sigmoid(
            zgate_ref[j] + b_in_ref[:, OFF_GATE + j * gate_w:OFF_GATE + (j + 1) * gate_w])

    zp = _dot(xn, w_in_ref[:, OFF_POOL:OFF_GATE]) + b_in_ref[:, OFF_POOL:OFF_GATE]
    h_a = _dot(c, cproj_ref[...])

    zg = jax.nn.gelu(zg)
    u = zg[:, :D_BRANCH]
    v = _layernorm(zg[:, D_BRANCH:], glg_ref[...], glb_ref[...]).astype(jnp.bfloat16)
    ri = lax.broadcasted_iota(jnp.int32, (GMLP_BLOCK, GMLP_BLOCK), 0) // CHUNK
    ci = lax.broadcasted_iota(jnp.int32, (GMLP_BLOCK, GMLP_BLOCK), 1) // CHUNK
    wsm = [jnp.where(ci <= ri, ws_ref[h], 0.0).astype(jnp.bfloat16) for h in range(GMLP_HEADS)]
    lane = lax.broadcasted_iota(jnp.int32, (GMLP_BLOCK, MXU_N), 1)
    sv_rows = []
    for n in range(rows // GMLP_BLOCK):
        sv_tiles = []
        for t in range(D_BRANCH // MXU_N):
            h0 = (t * MXU_N) // GMLP_HD
            split = (h0 + 1) * GMLP_HD - t * MXU_N
            vt = v[n * GMLP_BLOCK:(n + 1) * GMLP_BLOCK, t * MXU_N:(t + 1) * MXU_N]
            zero = jnp.zeros_like(vt)
            rhs = jnp.concatenate([jnp.where(lane < split, vt, zero),
                                   jnp.where(lane < split, zero, vt)], axis=0)
            lhs = jnp.concatenate([wsm[h0], wsm[h0 + 1]], axis=1)
            sv_tiles.append(_dot(lhs, rhs))
        sv_rows.append(jnp.concatenate(sv_tiles, axis=1) + bsf_ref[...])
    sv = jnp.concatenate(sv_rows, axis=0)

    lane1 = lax.broadcasted_iota(jnp.int32, (rows, LANES), 1)
    wsum_tiles = []
    for j in range(N_BT):
        pbuf[j, POOL_HALO:POOL_HALO + rows, :] = zp[:, j * LANES:(j + 1) * LANES]
        parts = _pool_window_of_lane_tile(j)
        wmax = max(w for w, _ in parts)
        run = None
        sums = {}
        for i in range(wmax):
            term = pbuf[j, pl.ds(POOL_HALO - i, rows), :]
            run = term if run is None else run + term
            sums[i + 1] = run
        tile = sums[parts[0][0]]
        for w, first in parts[1:]:
            tile = jnp.where(lane1 < first, tile, sums[w])
        wsum_tiles.append(tile)
        pbuf[j, 0:POOL_HALO, :] = pbuf[j, rows:rows + POOL_HALO, :]
    wsum = jnp.concatenate(wsum_tiles, axis=-1)
    head_inv = jnp.where(s == 0, icnt_ref[...], jnp.broadcast_to(winv_ref[...], (POOL_HALO, D_BRANCH)))
    avg = jnp.concatenate([wsum[:POOL_HALO] * head_inv, wsum[POOL_HALO:] * winv_ref[...]], axis=0)
    pooled = (avg - zp).astype(jnp.bfloat16)
    mixed = _dot(pooled, pw_ref[...]) * pscale_ref[...]
    h_b = _dot((u * sv).astype(jnp.bfloat16), gproj_ref[...])
    h_c = _dot(mixed.astype(jnp.bfloat16), pproj_ref[...])

    def gate(branch, kc):
        col = branch * D_MODEL + kc * MXU_N
        return zgate_ref[col // gate_w, :, col % gate_w:col % gate_w + MXU_N]

    out = x
    for kc in range(D_MODEL // MXU_N):
        sl = slice(kc * MXU_N, (kc + 1) * MXU_N)
        y = gate(0, kc) * h_a[:, sl] + gate(1, kc) * h_b[:, sl] + gate(2, kc) * h_c[:, sl]
        out = out + _dot(y.astype(jnp.bfloat16), wout_ref[sl, :])
    o_ref[0] = out


def _ffn_kernel(x_ref, ng_ref, wup_ref, fcw_ref, wdown_ref, fg_ref, o_ref, hbuf, *, final_norm):
    rows = x_ref.shape[1]
    s = pl.program_id(1)

    @pl.when(s == 0)
    def _():
        hbuf[:, 0:FFN_HALO, :] = jnp.zeros((N_FT, FFN_HALO, LANES), jnp.float32)

    x = x_ref[0]
    xn = _rmsnorm(x, ng_ref[...]).astype(jnp.bfloat16)
    h = _dot(xn, wup_ref[...])
    conv_tiles = []
    for j in range(N_FT):
        hbuf[j, FFN_HALO:FFN_HALO + rows, :] = h[:, j * LANES:(j + 1) * LANES]
        acc = None
        for k in range(FFN_K):
            off = FFN_HALO - (FFN_K - 1) + k
            term = hbuf[j, pl.ds(off, rows), :] * fcw_ref[k:k + 1, j * LANES:(j + 1) * LANES]
            acc = term if acc is None else acc + term
        conv_tiles.append(acc)
        hbuf[j, 0:FFN_HALO, :] = hbuf[j, rows:rows + FFN_HALO, :]
    hc = jnp.concatenate(conv_tiles, axis=-1)
    act = (jax.nn.gelu(hc[:, :D_FF]) * hc[:, D_FF:]).astype(jnp.bfloat16)
    out = x + _dot(act, wdown_ref[...])
    if final_norm:
        out = _rmsnorm(out, fg_ref[...])
    o_ref[0] = out


def _const_spec(shape):
    return pl.BlockSpec(shape, lambda b, s: (0,) * len(shape), pipeline_mode=pl.Buffered(1))


def _call(body, x, consts, scratch, name):
    batch, seq, d = x.shape
    act_spec = pl.BlockSpec((1, ROWS, d), lambda b, s: (b, s, 0))
    return pl.pallas_call(
        body,
        grid=(batch, seq // ROWS),
        in_specs=[act_spec] + [_const_spec(c.shape) for c in consts],
        out_specs=act_spec,
        out_shape=jax.ShapeDtypeStruct(x.shape, x.dtype),
        scratch_shapes=scratch,
        compiler_params=pltpu.CompilerParams(
            dimension_semantics=("arbitrary", "arbitrary"),
            vmem_limit_bytes=VMEM_LIMIT),
        name=name,
    )(x, *consts)


def _pool_tables():
    w_of_ch = jnp.repeat(jnp.asarray(POOL_WINDOWS, jnp.float32), POOL_GD)[None, :]
    pos1 = jnp.arange(1, POOL_HALO + 1, dtype=jnp.float32)[:, None]
    return 1.0 / w_of_ch, 1.0 / jnp.minimum(pos1, w_of_ch)


def kernel(x, attn_norm_g, w_in, b_in, conv_w, conv_ln_g, conv_ln_b, conv_proj, gmlp_ln_g, gmlp_ln_b, gmlp_ws, gmlp_bs, gmlp_proj, pool_w, pool_scale, pool_proj, w_out, ffn_norm_g, w_up, ffn_conv_w, w_down, final_norm_g):
    depth = w_in.shape[0]
    bf = jnp.bfloat16
    winv, icnt = _pool_tables()
    row = lambda v: v[None, :]
    for l in range(depth):
        bsf = jnp.repeat(gmlp_bs[l].T, GMLP_HD, axis=1)
        pw = jax.scipy.linalg.block_diag(*[pool_w[l, g] for g in range(len(POOL_WINDOWS))])
        gate_w = (N_IN - OFF_GATE) // N_BT
        wgate = w_in[l, :, OFF_GATE:].astype(bf).reshape(D_MODEL, N_BT, gate_w).transpose(1, 0, 2)
        cw = conv_w[l].reshape(CONV_K, N_BT, LANES).transpose(1, 0, 2)
        mixer_consts = [
            row(attn_norm_g[l]), w_in[l, :, :OFF_GATE].astype(bf), wgate, row(b_in[l]), cw,
            row(conv_ln_g[l]), row(conv_ln_b[l]), conv_proj[l].astype(bf),
            row(gmlp_ln_g[l]), row(gmlp_ln_b[l]), gmlp_ws[l], bsf, gmlp_proj[l].astype(bf),
            pw.astype(bf), row(pool_scale[l]), winv, icnt, pool_proj[l].astype(bf),
            w_out[l].astype(bf)]
        x = _call(_mixer_kernel, x, mixer_consts,
                  [pltpu.VMEM((N_BT, ROWS + CONV_HALO, LANES), jnp.float32),
                   pltpu.VMEM((N_BT, ROWS + POOL_HALO, LANES), jnp.float32),
                   pltpu.VMEM((N_BT, ROWS, gate_w), jnp.float32),
                   pltpu.VMEM((ROWS, D_MODEL), jnp.bfloat16)],
                  name=f"mixer{l}")
        ffn_consts = [row(ffn_norm_g[l]), w_up[l].astype(bf), ffn_conv_w[l],
                      w_down[l].astype(bf), row(final_norm_g)]
        x = _call(functools.partial(_ffn_kernel, final_norm=(l == depth - 1)), x, ffn_consts,
                  [pltpu.VMEM((N_FT, ROWS + FFN_HALO, LANES), jnp.float32)],
                  name=f"ffn{l}")
    return x
```

```python
import functools

import jax
import jax.numpy as jnp
from jax import lax
from jax.experimental import pallas as pl
from jax.experimental.pallas import tpu as pltpu

D_MODEL = 1024
D_BRANCH = 768
CONV_K = 31
GMLP_BLOCK = 128
GMLP_HEADS = 4
GMLP_HD = D_BRANCH // GMLP_HEADS
CHUNK = 64
POOL_WINDOWS = (2, 4, 8, 16)
POOL_GD = D_BRANCH // len(POOL_WINDOWS)
D_FF = 2 * D_MODEL
FFN_K = 3
EPS = 1e-6
OFF_GMLP = 2 * D_BRANCH
OFF_POOL = 4 * D_BRANCH
OFF_GATE = 5 * D_BRANCH
N_IN = 5 * D_BRANCH + 3 * D_MODEL

LANES = 128
N_BT = D_BRANCH // LANES
N_FT = 2 * D_FF // LANES
CONV_HALO = 32
POOL_HALO = 16
FFN_HALO = 8
MXU_N = 256
ROWS = 512
CONV_ROWS = 64
VMEM_LIMIT = 56 * 1024 * 1024


def _rmsnorm(x, g):
    ms = jnp.mean(x * x, axis=-1, keepdims=True)
    return x * lax.rsqrt(ms + EPS) * g


def _layernorm(x, g, b):
    mu = jnp.mean(x, axis=-1, keepdims=True)
    xc = x - mu
    var = jnp.mean(xc * xc, axis=-1, keepdims=True)
    return xc * lax.rsqrt(var + EPS) * g + b


_dot = functools.partial(jnp.dot, preferred_element_type=jnp.float32)


def _exact_zero_of(x, shape):
    r, w = x.shape
    bits = lax.bitcast_convert_type(x, jnp.uint32)
    tiles = [bits[i:i + 8, l:l + LANES] for i in range(0, r, 8) for l in range(0, w, LANES)]
    while len(tiles) > 1:
        tiles = [a | b for a, b in zip(tiles[0::2], tiles[1::2])] + tiles[len(tiles) & ~1:]
    zero = lax.bitcast_convert_type((tiles[0] >> 16) >> 16, jnp.float32)
    return jnp.tile(zero, (shape[0] // 8, shape[1] // LANES))


def _pool_window_of_lane_tile(j):
    lo, hi = j * LANES, (j + 1) * LANES
    out = []
    for g, w in enumerate(POOL_WINDOWS):
        a, b = max(lo, g * POOL_GD), min(hi, (g + 1) * POOL_GD)
        if a < b:
            out.append((w, a - lo))
    return out


def _pool_fold_kernel(pw_ref, pscale_ref, pproj_ref, o_ref):
    w1 = pw_ref[...] * pscale_ref[...]
    o_ref[...] = jnp.dot(w1, pproj_ref[...], precision=lax.Precision.HIGHEST,
                         preferred_element_type=jnp.float32).astype(o_ref.dtype)


def _mixer_kernel(x_ref, ng_ref, w_in_ref, b_in_ref, cw_ref, clg_ref, clb_ref, cproj_ref,
                  glg_ref, glb_ref, ws_ref, bsf_ref, gproj_ref,
                  winv_ref, icnt_ref, pfold_ref, wout_ref,
                  o_ref, abuf, pbuf, zraw_ref, xn_ref):
    rows = x_ref.shape[1]
    s = pl.program_id(1)

    @pl.when(s == 0)
    def _():
        abuf[:, 0:CONV_HALO, :] = jnp.zeros((N_BT, CONV_HALO, LANES), jnp.float32)
        pbuf[:, 0:POOL_HALO, :] = jnp.zeros((N_BT, POOL_HALO, LANES), jnp.float32)

    x = x_ref[0]
    xn_ref[...] = _rmsnorm(x, ng_ref[...]).astype(jnp.bfloat16)
    xn = xn_ref[...]

    zc = _dot(xn, w_in_ref[:, 0:OFF_GMLP]) + b_in_ref[:, 0:OFF_GMLP]
    a = zc[:, :D_BRANCH] * jax.nn.sigmoid(zc[:, D_BRANCH:])
    for j in range(N_BT):
        abuf[j, CONV_HALO:CONV_HALO + rows, :] = a[:, j * LANES:(j + 1) * LANES]
    gate_w = (N_IN - OFF_GATE) // N_BT

    def raw_cols(lo, hi):
        return zraw_ref.at[:, lo - OFF_GATE:hi - OFF_GATE]

    conv_tiles = []
    for j in range(N_BT):
        lo = OFF_GATE + j * gate_w
        raw = _dot(xn, w_in_ref[:, lo:lo + gate_w])
        raw_cols(lo, lo + gate_w)[...] = raw
        chunks = []
        for r in range(0, rows, CONV_ROWS):
            acc = None
            for k in range(CONV_K):
                off = CONV_HALO - (CONV_K - 1) + k + r
                term = (abuf[j, pl.ds(off, CONV_ROWS), :]
                        * cw_ref[k:k + 1, j * LANES:(j + 1) * LANES])
                acc = term if acc is None else acc + term
            chunks.append(acc)
        chunks[0] = chunks[0] + _exact_zero_of(raw, chunks[0].shape)
        conv_tiles.append(jnp.concatenate(chunks, axis=0))
        abuf[j, 0:CONV_HALO, :] = abuf[j, rows:rows + CONV_HALO, :]

    zg = _dot(xn, w_in_ref[:, OFF_GMLP:OFF_POOL]) + b_in_ref[:, OFF_GMLP:OFF_POOL]
    c = jnp.concatenate(conv_tiles, axis=-1)
    c = jax.nn.silu(_layernorm(c, clg_ref[...], clb_ref[...])).astype(jnp.bfloat16)
    for lo in range(OFF_GATE, N_IN, gate_w):
        g_ref = raw_cols(lo, lo + gate_w)
        g_ref[...] = jax.nn.sigmoid(g_ref[...] + b_in_ref[:, lo:lo + gate_w])

    zp = _dot(xn, w_in_ref[:, OFF_POOL:OFF_GATE]) + b_in_ref[:, OFF_POOL:OFF_GATE]
    h_a = _dot(c, cproj_ref[...])

    zg = jax.nn.gelu(zg)
    u = zg[:, :D_BRANCH]
    v = _layernorm(zg[:, D_BRANCH:], glg_ref[...], glb_ref[...]).astype(jnp.bfloat16)
    ri = lax.broadcasted_iota(jnp.int32, (GMLP_BLOCK, GMLP_BLOCK), 0) // CHUNK
    ci = lax.broadcasted_iota(jnp.int32, (GMLP_BLOCK, GMLP_BLOCK), 1) // CHUNK
    wsm = [jnp.where(ci <= ri, ws_ref[h], 0.0).astype(jnp.bfloat16) for h in range(GMLP_HEADS)]
    lane = lax.broadcasted_iota(jnp.int32, (GMLP_BLOCK, MXU_N), 1)
    sv_rows = []
    for n in range(rows // GMLP_BLOCK):
        sv_tiles = []
        for t in range(D_BRANCH // MXU_N):
            h0 = (t * MXU_N) // GMLP_HD
            split = (h0 + 1) * GMLP_HD - t * MXU_N
            vt = v[n * GMLP_BLOCK:(n + 1) * GMLP_BLOCK, t * MXU_N:(t + 1) * MXU_N]
            zero = jnp.zeros_like(vt)
            rhs = jnp.concatenate([jnp.where(lane < split, vt, zero),
                                   jnp.where(lane < split, zero, vt)], axis=0)
            lhs = jnp.concatenate([wsm[h0], wsm[h0 + 1]], axis=1)
            sv_tiles.append(_dot(lhs, rhs))
        sv_rows.append(jnp.concatenate(sv_tiles, axis=1) + bsf_ref[...])
    sv = jnp.concatenate(sv_rows, axis=0)

    lane1 = lax.broadcasted_iota(jnp.int32, (rows, LANES), 1)
    wsum_tiles = []
    for j in range(N_BT):
        pbuf[j, POOL_HALO:POOL_HALO + rows, :] = zp[:, j * LANES:(j + 1) * LANES]
        parts = _pool_window_of_lane_tile(j)
        wmax = max(w for w, _ in parts)
        run = None
        sums = {}
        for i in range(wmax):
            term = pbuf[j, pl.ds(POOL_HALO - i, rows), :]
            run = term if run is None else run + term
            sums[i + 1] = run
        tile = sums[parts[0][0]]
        for w, first in parts[1:]:
            tile = jnp.where(lane1 < first, tile, sums[w])
        wsum_tiles.append(tile)
        pbuf[j, 0:POOL_HALO, :] = pbuf[j, rows:rows + POOL_HALO, :]
    wsum = jnp.concatenate(wsum_tiles, axis=-1)
    head_inv = jnp.where(s == 0, icnt_ref[...], jnp.broadcast_to(winv_ref[...], (POOL_HALO, D_BRANCH)))
    avg = jnp.concatenate([wsum[:POOL_HALO] * head_inv, wsum[POOL_HALO:] * winv_ref[...]], axis=0)
    pooled = (avg - zp).astype(jnp.bfloat16)
    h_b = _dot((u * sv).astype(jnp.bfloat16), gproj_ref[...])
    h_c = _dot(pooled, pfold_ref[...])

    def gate(branch, kc):
        col = OFF_GATE + branch * D_MODEL + kc * MXU_N
        return raw_cols(col, col + MXU_N)[...]

    out = x
    for kc in range(D_MODEL // MXU_N):
        sl = slice(kc * MXU_N, (kc + 1) * MXU_N)
        y = gate(0, kc) * h_a[:, sl] + gate(1, kc) * h_b[:, sl] + gate(2, kc) * h_c[:, sl]
        out = out + _dot(y.astype(jnp.bfloat16), wout_ref[sl, :])
    o_ref[0] = out


def _ffn_kernel(x_ref, ng_ref, wup_ref, fcw_ref, wdown_ref, fg_ref, o_ref, hbuf, *, final_norm):
    rows = x_ref.shape[1]
    s = pl.program_id(1)

    @pl.when(s == 0)
    def _():
        hbuf[:, 0:FFN_HALO, :] = jnp.zeros((N_FT, FFN_HALO, LANES), jnp.float32)

    x = x_ref[0]
    xn = _rmsnorm(x, ng_ref[...]).astype(jnp.bfloat16)
    h = _dot(xn, wup_ref[...])
    conv_tiles = []
    for j in range(N_FT):
        hbuf[j, FFN_HALO:FFN_HALO + rows, :] = h[:, j * LANES:(j + 1) * LANES]
        acc = None
        for k in range(FFN_K):
            off = FFN_HALO - (FFN_K - 1) + k
            term = hbuf[j, pl.ds(off, rows), :] * fcw_ref[k:k + 1, j * LANES:(j + 1) * LANES]
            acc = term if acc is None else acc + term
        conv_tiles.append(acc)
        hbuf[j, 0:FFN_HALO, :] = hbuf[j, rows:rows + FFN_HALO, :]
    hc = jnp.concatenate(conv_tiles, axis=-1)
    act = (jax.nn.gelu(hc[:, :D_FF]) * hc[:, D_FF:]).astype(jnp.bfloat16)
    out = x + _dot(act, wdown_ref[...])
    if final_norm:
        out = _rmsnorm(out, fg_ref[...])
    o_ref[0] = out


def _layer_spec(arr, layer):
    zeros = (0,) * (arr.ndim - 1)
    return pl.BlockSpec((None,) + arr.shape[1:], lambda b, s: (layer,) + zeros,
                        pipeline_mode=pl.Buffered(1))


def _call(body, x, layer, consts, scratch, name):
    batch, seq, d = x.shape
    act_spec = pl.BlockSpec((1, ROWS, d), lambda b, s: (b, s, 0))
    return pl.pallas_call(
        body,
        grid=(batch, seq // ROWS),
        in_specs=[act_spec] + [_layer_spec(c, layer) for c in consts],
        out_specs=act_spec,
        out_shape=jax.ShapeDtypeStruct(x.shape, x.dtype),
        scratch_shapes=scratch,
        compiler_params=pltpu.CompilerParams(
            dimension_semantics=("arbitrary", "arbitrary"),
            vmem_limit_bytes=VMEM_LIMIT),
        name=name,
    )(x, *consts)


def _pool_tables():
    w_of_ch = jnp.repeat(jnp.asarray(POOL_WINDOWS, jnp.float32), POOL_GD)[None, :]
    pos1 = jnp.arange(1, POOL_HALO + 1, dtype=jnp.float32)[:, None]
    return 1.0 / w_of_ch, 1.0 / jnp.minimum(pos1, w_of_ch)


def _fold_pool_maps(pool_w, pool_scale, pool_proj):
    depth = pool_w.shape[0]
    pw = jnp.zeros((depth, D_BRANCH, D_BRANCH), jnp.float32)
    for g in range(len(POOL_WINDOWS)):
        sl = slice(g * POOL_GD, (g + 1) * POOL_GD)
        pw = pw.at[:, sl, sl].set(pool_w[:, g])
    spec = lambda a: pl.BlockSpec((None,) + a.shape[1:], lambda l: (l,) + (0,) * (a.ndim - 1))
    args = (pw, pool_scale[:, None, :], pool_proj)
    return pl.pallas_call(
        _pool_fold_kernel,
        grid=(depth,),
        in_specs=[spec(a) for a in args],
        out_specs=pl.BlockSpec((None, D_BRANCH, D_MODEL), lambda l: (l, 0, 0)),
        out_shape=jax.ShapeDtypeStruct((depth, D_BRANCH, D_MODEL), jnp.bfloat16),
        name="pool_fold",
    )(*args)


def kernel(x, attn_norm_g, w_in, b_in, conv_w, conv_ln_g, conv_ln_b, conv_proj, gmlp_ln_g, gmlp_ln_b, gmlp_ws, gmlp_bs, gmlp_proj, pool_w, pool_scale, pool_proj, w_out, ffn_norm_g, w_up, ffn_conv_w, w_down, final_norm_g):
    depth = w_in.shape[0]
    bf = jnp.bfloat16
    winv, icnt = _pool_tables()
    rows3 = lambda v: v[:, None, :]
    per_layer = lambda v: jnp.broadcast_to(v[None], (depth,) + v.shape)
    bsf = jnp.repeat(jnp.swapaxes(gmlp_bs, 1, 2), GMLP_HD, axis=2)
    mixer_consts = [
        rows3(attn_norm_g), w_in.astype(bf), rows3(b_in), conv_w,
        rows3(conv_ln_g), rows3(conv_ln_b), conv_proj.astype(bf),
        rows3(gmlp_ln_g), rows3(gmlp_ln_b), gmlp_ws, bsf, gmlp_proj.astype(bf),
        per_layer(winv), per_layer(icnt),
        _fold_pool_maps(pool_w, pool_scale, pool_proj), w_out.astype(bf)]
    ffn_consts = [rows3(ffn_norm_g), w_up.astype(bf), ffn_conv_w, w_down.astype(bf),
                  per_layer(final_norm_g[None, :])]
    gate_w = (N_IN - OFF_GATE) // N_BT
    for l in range(depth):
        x = _call(_mixer_kernel, x, l, mixer_consts,
                  [pltpu.VMEM((N_BT, ROWS + CONV_HALO, LANES), jnp.float32),
                   pltpu.VMEM((N_BT, ROWS + POOL_HALO, LANES), jnp.float32),
                   pltpu.VMEM((ROWS, N_IN - OFF_GATE), jnp.float32),
                   pltpu.VMEM((ROWS, D_MODEL), jnp.bfloat16)],
                  name=f"mixer{l}")
        x = _call(functools.partial(_ffn_kernel, final_norm=(l == depth - 1)), x, l, ffn_consts,
                  [pltpu.VMEM((N_FT, ROWS + FFN_HALO, LANES), jnp.float32)],
                  name=f"ffn{l}")
    return x
```
